```python
import jax, jax.numpy as jnp
from jax import lax
import numpy as np

D_MODEL = 2048
BATCH = 8
SEQ = 4096
DEPTH = 4
DEC_BATCH = 4
DEC_SEQ = 8192
PAST_LEN = 128

RET_HEADS = 8
RET_HEAD_DIM = 128
RET_WIDTH = RET_HEADS * RET_HEAD_DIM
LRU_BLOCKS = 8
LRU_BLOCK = 128
LRU_WIDTH = LRU_BLOCKS * LRU_BLOCK
MIX_WIDTH = RET_WIDTH + LRU_WIDTH
IN_WIDTH = 4 * RET_WIDTH + 2 * LRU_WIDTH
CONV_WIDTH = 4
CONV_PAD_LEFT = 2
CONV_PAD_RIGHT = CONV_WIDTH - 1 - CONV_PAD_LEFT
LRU_C = 8.0
D_FF = -(-8 * D_MODEL // (3 * 256)) * 256
CHUNK = 128
ROPE_BASE = 10000.0
EPS = 1e-6

kernel_name = "hymba_retnet_rglru_bidir_encoder"


def _rmsnorm(x, gain):
    xf = x.astype(jnp.float32)
    y = xf * lax.rsqrt(jnp.mean(xf * xf, axis=-1, keepdims=True) + EPS)
    return (y * gain.astype(jnp.float32)).astype(x.dtype)


def _rope(x):
    S, d = x.shape[1], x.shape[-1]
    inv = ROPE_BASE ** (-jnp.arange(0, d, 2, dtype=jnp.float32) / d)
    ang = jnp.arange(S, dtype=jnp.float32)[:, None] * inv[None, :]
    cos = jnp.cos(ang)[None, :, None, :].astype(x.dtype)
    sin = jnp.sin(ang)[None, :, None, :].astype(x.dtype)
    x1, x2 = x[..., : d // 2], x[..., d // 2:]
    return jnp.concatenate([x1 * cos - x2 * sin, x2 * cos + x1 * sin], axis=-1)


def _retention(q, k, v):
    B, S, H, dk = q.shape
    dv = v.shape[-1]
    N = S // CHUNK
    dt = q.dtype
    q = q.reshape(B, N, CHUNK, H, dk)
    k = k.reshape(B, N, CHUNK, H, dk)
    v = v.reshape(B, N, CHUNK, H, dv)
    log_g = jnp.log1p(-(2.0 ** (-5.0 - jnp.arange(H, dtype=jnp.float32))))
    pos = jnp.arange(CHUNK, dtype=jnp.float32)
    d_intra = jnp.exp(log_g[:, None, None] * jnp.abs(pos[:, None] - pos[None, :])[None]).astype(dt)
    scores = jnp.einsum('bnihd,bnjhd->bnhij', q, k) * d_intra
    y = jnp.einsum('bnhij,bnjhe->bnihe', scores, v)
    k_f = k * jnp.exp(log_g[None, :] * (CHUNK - 1 - pos)[:, None]).astype(dt)[:, :, None]
    k_b = k * jnp.exp(log_g[None, :] * pos[:, None]).astype(dt)[:, :, None]
    kv_f = jnp.einsum('bnjhd,bnjhe->nbhde', k_f, v)
    kv_b = jnp.einsum('bnjhd,bnjhe->nbhde', k_b, v)
    g_chunk = jnp.exp(log_g * CHUNK).astype(dt)[None, :, None, None]

    def step(state, kv):
        return state * g_chunk + kv, state

    zero = jnp.zeros((B, H, dk, dv), dt)
    _, s_f = lax.scan(step, zero, kv_f)
    _, s_b = lax.scan(step, zero, kv_b, reverse=True)
    q_f = q * jnp.exp(log_g[None, :] * (pos + 1.0)[:, None]).astype(dt)[:, :, None]
    q_b = q * jnp.exp(log_g[None, :] * (CHUNK - pos)[:, None]).astype(dt)[:, :, None]
    y = (y + jnp.einsum('bnihd,nbhde->bnihe', q_f, s_f)
         + jnp.einsum('bnihd,nbhde->bnihe', q_b, s_b))
    return y.reshape(B, S, H, dv)


def _centred_conv(u, w, bias):
    S = u.shape[1]
    up = jnp.pad(u, ((0, 0), (CONV_PAD_LEFT, CONV_PAD_RIGHT), (0, 0)))
    return bias + sum(up[:, t:t + S] * w[t] for t in range(CONV_WIDTH))


def _rg_lru_terms(u, w_a, b_a, w_x, b_x, lam):
    B, S, W = u.shape
    ub = u.reshape(B, S, LRU_BLOCKS, LRU_BLOCK)
    r = jax.nn.sigmoid(jnp.einsum('bsgi,gij->bsgj', ub, w_a).reshape(B, S, W) + b_a)
    i = jax.nn.sigmoid(jnp.einsum('bsgi,gij->bsgj', ub, w_x).reshape(B, S, W) + b_x)
    log_a = -LRU_C * r * jax.nn.softplus(-lam)
    a = jnp.exp(log_a)
    b = jnp.sqrt(-jnp.expm1(2.0 * log_a)) * (i * u)
    return a, b


def _combine(e1, e2):
    a1, b1 = e1
    a2, b2 = e2
    return a1 * a2, a2 * b1 + b2


def _linear_scan(a, b, reverse):
    if reverse:
        a, b = jnp.flip(a, 1), jnp.flip(b, 1)
    _, h = lax.associative_scan(_combine, (a, b), axis=1)
    return jnp.flip(h, 1) if reverse else h


def _layer(x, norm_pre_mix, w_in, ret_norm, conv_w, conv_b,
           lru_wa_fwd, lru_ba_fwd, lru_wx_fwd, lru_bx_fwd, lru_lam_fwd,
           lru_wa_bwd, lru_ba_bwd, lru_wx_bwd, lru_bx_bwd, lru_lam_bwd,
           lru_norm, w_out, norm_post_mix, norm_pre_ffn, w_gate, w_up, w_down, norm_post_ffn):
    B, S, _ = x.shape
    h = _rmsnorm(x, norm_pre_mix)
    proj = h @ w_in
    q, k, v, g, u_x, u_y = jnp.split(
        proj, [RET_WIDTH, 2 * RET_WIDTH, 3 * RET_WIDTH, 4 * RET_WIDTH, 4 * RET_WIDTH + LRU_WIDTH], axis=-1)
    q = _rope(q.reshape(B, S, RET_HEADS, RET_HEAD_DIM))
    k = _rope(k.reshape(B, S, RET_HEADS, RET_HEAD_DIM)) * (RET_HEAD_DIM ** -0.5)
    v = v.reshape(B, S, RET_HEADS, RET_HEAD_DIM)
    ret = _rmsnorm(_retention(q, k, v), ret_norm).reshape(B, S, RET_WIDTH)
    ret = jax.nn.silu(g) * ret
    uc = _centred_conv(u_x, conv_w, conv_b)
    a_f, b_f = _rg_lru_terms(uc, lru_wa_fwd, lru_ba_fwd, lru_wx_fwd, lru_bx_fwd, lru_lam_fwd)
    a_b, b_b = _rg_lru_terms(uc, lru_wa_bwd, lru_ba_bwd, lru_wx_bwd, lru_bx_bwd, lru_lam_bwd)
    lru = _linear_scan(a_f, b_f, False) + _linear_scan(a_b, b_b, True)
    lru = _rmsnorm(lru, lru_norm) * jax.nn.gelu(u_y)
    mix = jnp.concatenate([ret, lru], axis=-1) @ w_out
    x = x + _rmsnorm(mix, norm_post_mix)
    h = _rmsnorm(x, norm_pre_ffn)
    f = (jax.nn.silu(h @ w_gate) * (h @ w_up)) @ w_down
    return x + _rmsnorm(f, norm_post_ffn)


def _trunk(x, params):
    for l in range(DEPTH):
        x = _layer(x, *[p[l] for p in params])
    return x


def setup_inputs(seed: int = 0) -> dict:
    key = jax.random.key(seed)
    ks = jax.random.split(key, 32)
    f32 = jnp.float32
    nrm = lambda k, shape, s: jax.random.normal(k, shape, f32) * s
    gain = lambda k, shape: 1.0 + 0.05 * jax.random.normal(k, shape, f32)

    def lam_init(k):
        u = jax.random.uniform(k, (DEPTH, LRU_WIDTH), f32, minval=0.9, maxval=0.999)
        a = u ** (1.0 / LRU_C)
        return jnp.log(a) - jnp.log1p(-a)

    return {
        "x_prompt": jax.random.normal(ks[0], (BATCH, SEQ, D_MODEL), f32),
        "x_sample": jax.random.normal(ks[1], (DEC_BATCH, DEC_SEQ, D_MODEL), f32),
        "norm_pre_mix": gain(ks[2], (DEPTH, D_MODEL)),
        "w_in": nrm(ks[3], (DEPTH, D_MODEL, IN_WIDTH), D_MODEL ** -0.5),
        "ret_norm": gain(ks[4], (DEPTH, RET_HEADS, RET_HEAD_DIM)),
        "conv_w": nrm(ks[5], (DEPTH, CONV_WIDTH, LRU_WIDTH), CONV_WIDTH ** -0.5),
        "conv_b": nrm(ks[6], (DEPTH, LRU_WIDTH), 0.01),
        "lru_wa_fwd": nrm(ks[7], (DEPTH, LRU_BLOCKS, LRU_BLOCK, LRU_BLOCK), LRU_BLOCK ** -0.5),
        "lru_ba_fwd": nrm(ks[8], (DEPTH, LRU_WIDTH), 0.01),
        "lru_wx_fwd": nrm(ks[9], (DEPTH, LRU_BLOCKS, LRU_BLOCK, LRU_BLOCK), LRU_BLOCK ** -0.5),
        "lru_bx_fwd": nrm(ks[10], (DEPTH, LRU_WIDTH), 0.01),
        "lru_lam_fwd": lam_init(ks[11]),
        "lru_wa_bwd": nrm(ks[12], (DEPTH, LRU_BLOCKS, LRU_BLOCK, LRU_BLOCK), LRU_BLOCK ** -0.5),
        "lru_ba_bwd": nrm(ks[13], (DEPTH, LRU_WIDTH), 0.01),
        "lru_wx_bwd": nrm(ks[14], (DEPTH, LRU_BLOCKS, LRU_BLOCK, LRU_BLOCK), LRU_BLOCK ** -0.5),
        "lru_bx_bwd": nrm(ks[15], (DEPTH, LRU_WIDTH), 0.01),
        "lru_lam_bwd": lam_init(ks[16]),
        "lru_norm": gain(ks[17], (DEPTH, LRU_WIDTH)),
        "w_out": nrm(ks[18], (DEPTH, MIX_WIDTH, D_MODEL), MIX_WIDTH ** -0.5),
        "norm_post_mix": gain(ks[19], (DEPTH, D_MODEL)),
        "norm_pre_ffn": gain(ks[20], (DEPTH, D_MODEL)),
        "w_gate": nrm(ks[21], (DEPTH, D_MODEL, D_FF), D_MODEL ** -0.5),
        "w_up": nrm(ks[22], (DEPTH, D_MODEL, D_FF), D_MODEL ** -0.5),
        "w_down": nrm(ks[23], (DEPTH, D_FF, D_MODEL), D_FF ** -0.5),
        "norm_post_ffn": gain(ks[24], (DEPTH, D_MODEL)),
    }


def reference(x_prompt, x_sample, norm_pre_mix, w_in, ret_norm, conv_w, conv_b,
              lru_wa_fwd, lru_ba_fwd, lru_wx_fwd, lru_bx_fwd, lru_lam_fwd,
              lru_wa_bwd, lru_ba_bwd, lru_wx_bwd, lru_bx_bwd, lru_lam_bwd,
              lru_norm, w_out, norm_post_mix, norm_pre_ffn, w_gate, w_up, w_down, norm_post_ffn):
    params = (norm_pre_mix, w_in, ret_norm, conv_w, conv_b,
              lru_wa_fwd, lru_ba_fwd, lru_wx_fwd, lru_bx_fwd, lru_lam_fwd,
              lru_wa_bwd, lru_ba_bwd, lru_wx_bwd, lru_bx_bwd, lru_lam_bwd,
              lru_norm, w_out, norm_post_mix, norm_pre_ffn, w_gate, w_up, w_down, norm_post_ffn)
    y_prompt = _trunk(x_prompt, params)
    y_sample = _trunk(x_sample, params)
    return (y_prompt, y_sample)
```

```python
import functools

import jax
import jax.numpy as jnp
from jax import lax
from jax.experimental import pallas as pl
from jax.experimental.pallas import tpu as pltpu

F32 = jnp.float32
BF16 = jnp.bfloat16

RET_HEADS = 8
HEAD_DIM = 128
RET_WIDTH = RET_HEADS * HEAD_DIM
LRU_BLOCKS = 8
LRU_BLOCK = 128
LRU_WIDTH = LRU_BLOCKS * LRU_BLOCK
CONV_WIDTH = 4
CONV_PAD_LEFT = 2
LRU_C = 8.0
ROPE_BASE = 10000.0
EPS = 1e-6

V7X_LANES = 128
V7X_SUBLANES = 8
V7X_VMEM_LIMIT_BYTES = 56 * 1024 * 1024

IN_TM = 1024
IN_TN = 1024
OUT_TM = 512
FFN_TM = 512
FFN_TF = 512
RET_CHUNK = 128
LRU_TT = 128
LRU_HALO = V7X_SUBLANES


def _params(*sem):
    return pltpu.CompilerParams(dimension_semantics=sem, vmem_limit_bytes=V7X_VMEM_LIMIT_BYTES)


def _rms(x, gain):
    ms = jnp.mean(x * x, axis=-1, keepdims=True)
    return x * lax.rsqrt(ms + EPS) * gain


def _sigmoid(x):
    return 0.5 * (1.0 + jnp.tanh(0.5 * x))


def _gelu_tanh(x):
    c = 0.7978845608028654
    return 0.5 * x * (1.0 + jnp.tanh(c * (x + 0.044715 * (x * x * x))))


def _inproj_kernel(x_ref, gain_ref, w_ref, cos_ref, sin_ref, o_ref, hs_ref):
    j = pl.program_id(1)

    @pl.when(j == 0)
    def _():
        hs_ref[...] = _rms(x_ref[...], gain_ref[...]).astype(BF16)

    acc = jnp.dot(hs_ref[...], w_ref[...], preferred_element_type=F32)

    @pl.when(j < 2)
    def _():
        scale = jnp.where(j == 1, HEAD_DIM ** -0.5, 1.0).astype(F32)
        cosf = cos_ref[...]
        sinf = sin_ref[...]
        for h in range(IN_TN // HEAD_DIM):
            sl = slice(h * HEAD_DIM, (h + 1) * HEAD_DIM)
            a = acc[:, sl]
            r = pltpu.roll(a, HEAD_DIM // 2, axis=1)
            o_ref[:, sl] = ((a * cosf + r * sinf) * scale).astype(BF16)

    @pl.when(j >= 2)
    def _():
        o_ref[...] = acc.astype(BF16)


def _inproj(x2, gain, w_in, cosf, sinf, seq):
    T, D = x2.shape
    N = w_in.shape[1]
    tm = min(IN_TM, seq)
    pos_blocks = seq // tm
    return pl.pallas_call(
        _inproj_kernel,
        grid=(T // tm, N // IN_TN),
        in_specs=[
            pl.BlockSpec((tm, D), lambda i, j: (i, 0)),
            pl.BlockSpec((1, D), lambda i, j: (0, 0)),
            pl.BlockSpec((D, IN_TN), lambda i, j: (0, j)),
            pl.BlockSpec((tm, HEAD_DIM), lambda i, j: (i % pos_blocks, 0)),
            pl.BlockSpec((tm, HEAD_DIM), lambda i, j: (i % pos_blocks, 0)),
        ],
        out_specs=pl.BlockSpec((tm, IN_TN), lambda i, j: (i, j)),
        out_shape=jax.ShapeDtypeStruct((T, N), BF16),
        scratch_shapes=[pltpu.VMEM((tm, D), BF16)],
        compiler_params=_params("parallel", "arbitrary"),
        name="inproj",
    )(x2, gain, w_in, cosf, sinf)


def _ret_kernel(q_ref, k_ref, v_ref, g_ref, dmat_ref, kdec_ref, qdec_ref, gch_ref, gain_ref,
                o_ref, sf_ref, sb_ref, *, chunk, n_chunks):
    C = chunk
    kf_row = kdec_ref[0:1, :]
    kb_row = kdec_ref[1:2, :]
    g_chunk = gch_ref[...]

    def rows(n):
        return pl.ds(pl.multiple_of(n * C, C), C)

    def summaries(n, carry):
        kT = k_ref[rows(n), :].astype(F32).T
        v = v_ref[rows(n), :]
        sf_ref[n] = jnp.dot((kT * kf_row).astype(BF16), v, preferred_element_type=F32)
        sb_ref[n] = jnp.dot((kT * kb_row).astype(BF16), v, preferred_element_type=F32)
        return carry

    lax.fori_loop(0, n_chunks, summaries, 0)

    def fwd(n, state):
        kv = sf_ref[n]
        sf_ref[n] = state
        return state * g_chunk + kv

    def bwd(i, state):
        n = n_chunks - 1 - i
        kv = sb_ref[n]
        sb_ref[n] = state
        return state * g_chunk + kv

    zero = jnp.zeros((HEAD_DIM, HEAD_DIM), F32)
    lax.fori_loop(0, n_chunks, fwd, zero)
    lax.fori_loop(0, n_chunks, bwd, zero)

    dmat = dmat_ref[...]
    qf_col = qdec_ref[:, 0:HEAD_DIM]
    qb_col = qdec_ref[:, HEAD_DIM:2 * HEAD_DIM]
    gain = gain_ref[...]

    def outputs(n, carry):
        q = q_ref[rows(n), :]
        k = k_ref[rows(n), :]
        v = v_ref[rows(n), :]
        s = lax.dot_general(q, k, (((1,), (1,)), ((), ())), preferred_element_type=F32) * dmat
        qf = q.astype(F32)
        y = jnp.dot(s.astype(BF16), v, preferred_element_type=F32)
        y += jnp.dot((qf * qf_col).astype(BF16), sf_ref[n].astype(BF16), preferred_element_type=F32)
        y += jnp.dot((qf * qb_col).astype(BF16), sb_ref[n].astype(BF16), preferred_element_type=F32)
        gate = g_ref[rows(n), :].astype(F32)
        o_ref[rows(n), :] = (gate * _sigmoid(gate) * _rms(y, gain)).astype(BF16)
        return carry

    lax.fori_loop(0, n_chunks, outputs, 0)


def _ret_tables(chunk):
    h = jnp.arange(RET_HEADS, dtype=F32)
    log_g = jnp.log1p(-(2.0 ** (-5.0 - h)))
    pos = jnp.arange(chunk, dtype=F32)
    dmat = jnp.exp(log_g[:, None, None] * jnp.abs(pos[:, None] - pos[None, :])[None])
    kdec = jnp.stack([jnp.exp(log_g[:, None] * (chunk - 1 - pos)[None, :]),
                      jnp.exp(log_g[:, None] * pos[None, :])], axis=1)
    qf = jnp.exp(log_g[:, None] * (pos + 1.0)[None, :])
    qb = jnp.exp(log_g[:, None] * (chunk - pos)[None, :])
    qdec = jnp.concatenate([jnp.broadcast_to(qf[:, :, None], (RET_HEADS, chunk, HEAD_DIM)),
                            jnp.broadcast_to(qb[:, :, None], (RET_HEADS, chunk, HEAD_DIM))], axis=-1)
    gch = jnp.broadcast_to(jnp.exp(log_g * chunk)[:, None, None], (RET_HEADS, 1, HEAD_DIM))
    return dmat, kdec, qdec, gch


def _retention(proj, ret_gain, tables, batch, seq):
    T = proj.shape[0]
    C = min(RET_CHUNK, seq)
    n_chunks = seq // C
    dmat, kdec, qdec, gch = tables
    H = RET_HEADS

    def col(off):
        return pl.BlockSpec((seq, HEAD_DIM), lambda b, h: (b, off + h))

    return pl.pallas_call(
        functools.partial(_ret_kernel, chunk=C, n_chunks=n_chunks),
        grid=(batch, H),
        in_specs=[
            col(0), col(H), col(2 * H), col(3 * H),
            pl.BlockSpec((None, C, C), lambda b, h: (h, 0, 0)),
            pl.BlockSpec((None, 2, C), lambda b, h: (h, 0, 0)),
            pl.BlockSpec((None, C, 2 * HEAD_DIM), lambda b, h: (h, 0, 0)),
            pl.BlockSpec((None, 1, HEAD_DIM), lambda b, h: (h, 0, 0)),
            pl.BlockSpec((None, 1, HEAD_DIM), lambda b, h: (h, 0, 0)),
        ],
        out_specs=pl.BlockSpec((seq, HEAD_DIM), lambda b, h: (b, h)),
        out_shape=jax.ShapeDtypeStruct((T, RET_WIDTH), BF16),
        scratch_shapes=[pltpu.VMEM((n_chunks, HEAD_DIM, HEAD_DIM), F32),
                        pltpu.VMEM((n_chunks, HEAD_DIM, HEAD_DIM), F32)],
        compiler_params=_params("parallel", "parallel"),
        name="retention",
    )(proj, proj, proj, proj, dmat, kdec, qdec, gch, ret_gain)


def _scan_rows(a, b, reverse):
    R = a.shape[0]
    a3 = a.reshape(R // V7X_SUBLANES, V7X_SUBLANES, V7X_LANES)
    b3 = b.reshape(R // V7X_SUBLANES, V7X_SUBLANES, V7X_LANES)
    row = lax.broadcasted_iota(jnp.int32, a3.shape, 1)
    for d in (1, 2, 4):
        if reverse:
            shift, keep = V7X_SUBLANES - d, row < V7X_SUBLANES - d
        else:
            shift, keep = d, row >= d
        a_s = jnp.where(keep, pltpu.roll(a3, shift, axis=1), 1.0)
        b_s = jnp.where(keep, pltpu.roll(b3, shift, axis=1), 0.0)
        b3 = a3 * b_s + b3
        a3 = a3 * a_s
    return a3, b3


def _scan_tile(a, b, h_in, reverse):
    a3, b3 = _scan_rows(a, b, reverse)
    n = a3.shape[0]
    out = [None] * n
    h = h_in
    order = range(n - 1, -1, -1) if reverse else range(n)
    edge = 0 if reverse else V7X_SUBLANES - 1
    for v in order:
        hv = b3[v] + a3[v] * h
        out[v] = hv
        h = hv[edge:edge + 1, :]
    return jnp.concatenate(out, axis=0), h


def _lru_kernel(u_ref, cw_ref, cb_ref, w4_ref, b4_ref, lam_ref, o_ref,
                uf_ref, af_ref, bf_ref, ab_ref, bb_ref, *, seq, tile):
    n_tiles = seq // tile
    W = LRU_BLOCK

    def rows(i):
        return pl.ds(pl.multiple_of(i * tile, tile), tile)

    uf_ref[0:LRU_HALO, :] = jnp.zeros((LRU_HALO, W), F32)
    uf_ref[LRU_HALO + seq:2 * LRU_HALO + seq, :] = jnp.zeros((LRU_HALO, W), F32)

    def widen(i, carry):
        uf_ref[pl.ds(pl.multiple_of(LRU_HALO + i * tile, V7X_SUBLANES), tile), :] = (
            u_ref[rows(i), :].astype(F32))
        return carry

    lax.fori_loop(0, n_tiles, widen, 0)

    cw = cw_ref[...]
    cb = cb_ref[...]
    w4 = w4_ref[...]
    b4 = b4_ref[...]
    lam = lam_ref[...]
    z = -lam
    softplus = jnp.maximum(z, 0.0) + jnp.log1p(jnp.exp(-jnp.abs(z)))
    neg_c_sp = -LRU_C * softplus

    def gates(i, carry):
        base = pl.multiple_of(i * tile, tile)
        uc = cb
        for t in range(CONV_WIDTH):
            off = LRU_HALO - CONV_PAD_LEFT + t
            uc = uc + cw[t:t + 1, :] * uf_ref[pl.ds(base + off, tile), :]
        zz = jnp.dot(uc.astype(BF16), w4, preferred_element_type=F32) + b4
        for d, (a_ref, b_ref) in enumerate(((af_ref, bf_ref), (ab_ref, bb_ref))):
            r = _sigmoid(zz[:, (2 * d) * W:(2 * d + 1) * W])
            g_in = _sigmoid(zz[:, (2 * d + 1) * W:(2 * d + 2) * W])
            log_a = r * neg_c_sp[d:d + 1, :]
            a = jnp.exp(log_a)
            one_minus_a2 = -jnp.tanh(log_a) * (a * a + 1.0)
            a_ref[rows(i), :] = a
            b_ref[rows(i), :] = jnp.sqrt(one_minus_a2) * (g_in * uc)
        return carry

    lax.fori_loop(0, n_tiles, gates, 0)

    def fwd(i, h):
        hs, h = _scan_tile(af_ref[rows(i), :], bf_ref[rows(i), :], h, reverse=False)
        bf_ref[rows(i), :] = hs
        return h

    def bwd(j, h):
        i = n_tiles - 1 - j
        hs, h = _scan_tile(ab_ref[rows(i), :], bb_ref[rows(i), :], h, reverse=True)
        o_ref[rows(i), :] = (hs + bf_ref[rows(i), :]).astype(BF16)
        return h

    h0 = jnp.zeros((1, W), F32)
    lax.fori_loop(0, n_tiles, fwd, h0)
    lax.fori_loop(0, n_tiles, bwd, h0)


def _lru(proj, conv_w, conv_b, w4, b4, lam2, batch, seq):
    T = proj.shape[0]
    G, W = LRU_BLOCKS, LRU_BLOCK
    tile = min(LRU_TT, seq)
    u_col = (4 * RET_WIDTH) // W
    return pl.pallas_call(
        functools.partial(_lru_kernel, seq=seq, tile=tile),
        grid=(batch, G),
        in_specs=[
            pl.BlockSpec((seq, W), lambda b, g: (b, u_col + g)),
            pl.BlockSpec((CONV_WIDTH, W), lambda b, g: (0, g)),
            pl.BlockSpec((None, 1, W), lambda b, g: (g, 0, 0)),
            pl.BlockSpec((None, W, 4 * W), lambda b, g: (g, 0, 0)),
            pl.BlockSpec((None, 1, 4 * W), lambda b, g: (g, 0, 0)),
            pl.BlockSpec((None, 2, W), lambda b, g: (g, 0, 0)),
        ],
        out_specs=pl.BlockSpec((seq, W), lambda b, g: (b, g)),
        out_shape=jax.ShapeDtypeStruct((T, LRU_WIDTH), BF16),
        scratch_shapes=[pltpu.VMEM((seq + 2 * LRU_HALO, W), F32)] + [pltpu.VMEM((seq, W), F32)] * 4,
        compiler_params=_params("parallel", "parallel"),
        name="rglru",
    )(proj, conv_w, conv_b, w4, b4, lam2)


def _outproj_kernel(ret_ref, lru_ref, uy_ref, x_ref, w_ref, lgain_ref, pgain_ref, o_ref):
    lru = _rms(lru_ref[...].astype(F32), lgain_ref[...]) * _gelu_tanh(uy_ref[...].astype(F32))
    mix = jnp.dot(ret_ref[...], w_ref[0:RET_WIDTH, :], preferred_element_type=F32)
    mix += jnp.dot(lru.astype(BF16), w_ref[RET_WIDTH:RET_WIDTH + LRU_WIDTH, :],
                   preferred_element_type=F32)
    o_ref[...] = x_ref[...] + _rms(mix, pgain_ref[...])


def _outproj(ret, lru, proj, x2, w_out, lru_gain, post_gain):
    T, D = x2.shape
    tm = min(OUT_TM, T)
    uy_col = (4 * RET_WIDTH + LRU_WIDTH) // LRU_WIDTH
    return pl.pallas_call(
        _outproj_kernel,
        grid=(T // tm,),
        in_specs=[
            pl.BlockSpec((tm, RET_WIDTH), lambda i: (i, 0)),
            pl.BlockSpec((tm, LRU_WIDTH), lambda i: (i, 0)),
            pl.BlockSpec((tm, LRU_WIDTH), lambda i: (i, uy_col)),
            pl.BlockSpec((tm, D), lambda i: (i, 0)),
            pl.BlockSpec((RET_WIDTH + LRU_WIDTH, D), lambda i: (0, 0)),
            pl.BlockSpec((1, LRU_WIDTH), lambda i: (0, 0)),
            pl.BlockSpec((1, D), lambda i: (0, 0)),
        ],
        out_specs=pl.BlockSpec((tm, D), lambda i: (i, 0)),
        out_shape=jax.ShapeDtypeStruct((T, D), F32),
        compiler_params=_params("parallel"),
        name="outproj",
    )(ret, lru, proj, x2, w_out, lru_gain, post_gain)


def _ffn_kernel(x_ref, pre_ref, wg_ref, wu_ref, wd_ref, post_ref, o_ref, hs_ref, acc_ref):
    f = pl.program_id(1)

    @pl.when(f == 0)
    def _():
        hs_ref[...] = _rms(x_ref[...], pre_ref[...]).astype(BF16)
        acc_ref[...] = jnp.zeros_like(acc_ref)

    h = hs_ref[...]
    gate = jnp.dot(h, wg_ref[...], preferred_element_type=F32)
    up = jnp.dot(h, wu_ref[...], preferred_element_type=F32)
    act = (gate * _sigmoid(gate) * up).astype(BF16)
    acc_ref[...] += jnp.dot(act, wd_ref[...], preferred_element_type=F32)

    @pl.when(f == pl.num_programs(1) - 1)
    def _():
        o_ref[...] = x_ref[...] + _rms(acc_ref[...], post_ref[...])


def _ffn(x2, pre_gain, w_gate, w_up, w_down, post_gain):
    T, D = x2.shape
    F = w_gate.shape[1]
    tm = min(FFN_TM, T)
    tf = FFN_TF
    return pl.pallas_call(
        _ffn_kernel,
        grid=(T // tm, F // tf),
        in_specs=[
            pl.BlockSpec((tm, D), lambda i, f: (i, 0)),
            pl.BlockSpec((1, D), lambda i, f: (0, 0)),
            pl.BlockSpec((D, tf), lambda i, f: (0, f)),
            pl.BlockSpec((D, tf), lambda i, f: (0, f)),
            pl.BlockSpec((tf, D), lambda i, f: (f, 0)),
            pl.BlockSpec((1, D), lambda i, f: (0, 0)),
        ],
        out_specs=pl.BlockSpec((tm, D), lambda i, f: (i, 0)),
        out_shape=jax.ShapeDtypeStruct((T, D), F32),
        scratch_shapes=[pltpu.VMEM((tm, D), BF16), pltpu.VMEM((tm, D), F32)],
        compiler_params=_params("parallel", "arbitrary"),
        name="swiglu",
    )(x2, pre_gain, w_gate, w_up, w_down, post_gain)


def _rope_tables(seq):
    d = HEAD_DIM
    inv = ROPE_BASE ** (-jnp.arange(0, d, 2, dtype=F32) / d)
    ang = jnp.arange(seq, dtype=F32)[:, None] * inv[None, :]
    cos, sin = jnp.cos(ang), jnp.sin(ang)
    return jnp.concatenate([cos, cos], axis=-1), jnp.concatenate([-sin, sin], axis=-1)


def _prepare_layer(l, norm_pre_mix, w_in, ret_norm, conv_w, conv_b,
                   lru_wa_fwd, lru_ba_fwd, lru_wx_fwd, lru_bx_fwd, lru_lam_fwd,
                   lru_wa_bwd, lru_ba_bwd, lru_wx_bwd, lru_bx_bwd, lru_lam_bwd,
                   lru_norm, w_out, norm_post_mix, norm_pre_ffn, w_gate, w_up, w_down, norm_post_ffn):
    G, W = LRU_BLOCKS, LRU_BLOCK
    grp = lambda v: v[l].reshape(G, 1, W)
    return dict(
        pre_mix=norm_pre_mix[l][None, :],
        w_in=w_in[l].astype(BF16),
        ret_gain=ret_norm[l][:, None, :],
        conv_w=conv_w[l],
        conv_b=grp(conv_b),
        w4=jnp.concatenate([lru_wa_fwd[l], lru_wx_fwd[l], lru_wa_bwd[l], lru_wx_bwd[l]],
                           axis=-1).astype(BF16),
        b4=jnp.concatenate([grp(lru_ba_fwd), grp(lru_bx_fwd), grp(lru_ba_bwd), grp(lru_bx_bwd)],
                           axis=-1),
        lam2=jnp.concatenate([grp(lru_lam_fwd), grp(lru_lam_bwd)], axis=1),
        lru_gain=lru_norm[l][None, :],
        w_out=w_out[l].astype(BF16),
        post_mix=norm_post_mix[l][None, :],
        pre_ffn=norm_pre_ffn[l][None, :],
        w_gate=w_gate[l].astype(BF16),
        w_up=w_up[l].astype(BF16),
        w_down=w_down[l].astype(BF16),
        post_ffn=norm_post_ffn[l][None, :],
    )


def _trunk(x, layers, ret_tables):
    B, S, D = x.shape
    cosf, sinf = _rope_tables(S)
    x2 = x.reshape(B * S, D)
    for p in layers:
        proj = _inproj(x2, p["pre_mix"], p["w_in"], cosf, sinf, S)
        ret = _retention(proj, p["ret_gain"], ret_tables, B, S)
        lru = _lru(proj, p["conv_w"], p["conv_b"], p["w4"], p["b4"], p["lam2"], B, S)
        x2 = _outproj(ret, lru, proj, x2, p["w_out"], p["lru_gain"], p["post_mix"])
        x2 = _ffn(x2, p["pre_ffn"], p["w_gate"], p["w_up"], p["w_down"], p["post_ffn"])
    return x2.reshape(B, S, D)


def kernel(x_prompt, x_sample, norm_pre_mix, w_in, ret_norm, conv_w, conv_b, lru_wa_fwd, lru_ba_fwd, lru_wx_fwd, lru_bx_fwd, lru_lam_fwd, lru_wa_bwd, lru_ba_bwd, lru_wx_bwd, lru_bx_bwd, lru_lam_bwd, lru_norm, w_out, norm_post_mix, norm_pre_ffn, w_gate, w_up, w_down, norm_post_ffn):
    weights = (norm_pre_mix, w_in, ret_norm, conv_w, conv_b,
               lru_wa_fwd, lru_ba_fwd, lru_wx_fwd, lru_bx_fwd, lru_lam_fwd,
               lru_wa_bwd, lru_ba_bwd, lru_wx_bwd, lru_bx_bwd, lru_lam_bwd,
               lru_norm, w_out, norm_post_mix, norm_pre_ffn, w_gate, w_up, w_down, norm_post_ffn)
    depth = w_in.shape[0]
    layers = [_prepare_layer(l, *weights) for l in range(depth)]
    ret_tables = _ret_tables(RET_CHUNK)
    return (_trunk(x_prompt, layers, ret_tables), _trunk(x_sample, layers, ret_tables))
```

```python
import functools

import jax
import jax.numpy as jnp
from jax import lax
from jax.experimental import pallas as pl
from jax.experimental.pallas import tpu as pltpu

F32 = jnp.float32
BF16 = jnp.bfloat16

RET_HEADS = 8
HEAD_DIM = 128
RET_WIDTH = RET_HEADS * HEAD_DIM
LRU_BLOCKS = 8
LRU_BLOCK = 128
LRU_WIDTH = LRU_BLOCKS * LRU_BLOCK
CONV_WIDTH = 4
CONV_PAD_LEFT = 2
LRU_C = 8.0
ROPE_BASE = 10000.0
EPS = 1e-6

V7X_LANES = 128
V7X_SUBLANES = 8
V7X_VMEM_LIMIT_BYTES = 56 * 1024 * 1024

IN_TM = 1024
IN_TN = 1024
OUT_TM = 512
FFN_TM = 512
FFN_TF = 512
RET_CHUNK = 128
RET_UNROLL = 8
LRU_SEGS = V7X_SUBLANES
LRU_LEAD = V7X_SUBLANES
LRU_SLOT_PAD = 24
LRU_TT = 16
LRU_TW = 128
LRU_UNROLL = 2
NEG_LOG2_E = -1.4426950408889634


def _params(*sem):
    return pltpu.CompilerParams(dimension_semantics=sem, vmem_limit_bytes=V7X_VMEM_LIMIT_BYTES)


def _rms(x, gain):
    ms = jnp.mean(x * x, axis=-1, keepdims=True)
    return x * lax.rsqrt(ms + EPS) * gain


def _sigmoid(x):
    return 0.5 * (1.0 + jnp.tanh(0.5 * x))


def _gelu_tanh(x):
    c = 0.7978845608028654
    return 0.5 * x * (1.0 + jnp.tanh(c * (x + 0.044715 * (x * x * x))))


def _inproj_kernel(x_ref, gain_ref, w_ref, cos_ref, sin_ref, o_ref, hs_ref):
    j = pl.program_id(1)

    @pl.when(j == 0)
    def _():
        hs_ref[...] = _rms(x_ref[...], gain_ref[...]).astype(BF16)

    acc = jnp.dot(hs_ref[...], w_ref[...], preferred_element_type=F32)

    @pl.when(j < 2)
    def _():
        scale = jnp.where(j == 1, HEAD_DIM ** -0.5, 1.0).astype(F32)
        cosf = cos_ref[...]
        sinf = sin_ref[...]
        for h in range(IN_TN // HEAD_DIM):
            sl = slice(h * HEAD_DIM, (h + 1) * HEAD_DIM)
            a = acc[:, sl]
            r = pltpu.roll(a, HEAD_DIM // 2, axis=1)
            o_ref[:, sl] = ((a * cosf + r * sinf) * scale).astype(BF16)

    @pl.when(j >= 2)
    def _():
        o_ref[...] = acc.astype(BF16)


def _inproj(x2, gain, w_in, cosf, sinf, seq):
    T, D = x2.shape
    N = w_in.shape[1]
    tm = min(IN_TM, seq)
    pos_blocks = seq // tm
    return pl.pallas_call(
        _inproj_kernel,
        grid=(T // tm, N // IN_TN),
        in_specs=[
            pl.BlockSpec((tm, D), lambda i, j: (i, 0)),
            pl.BlockSpec((1, D), lambda i, j: (0, 0)),
            pl.BlockSpec((D, IN_TN), lambda i, j: (0, j)),
            pl.BlockSpec((tm, HEAD_DIM), lambda i, j: (i % pos_blocks, 0)),
            pl.BlockSpec((tm, HEAD_DIM), lambda i, j: (i % pos_blocks, 0)),
        ],
        out_specs=pl.BlockSpec((tm, IN_TN), lambda i, j: (i, j)),
        out_shape=jax.ShapeDtypeStruct((T, N), BF16),
        scratch_shapes=[pltpu.VMEM((tm, D), BF16)],
        compiler_params=_params("parallel", "arbitrary"),
        name="inproj",
    )(x2, gain, w_in, cosf, sinf)


def _ret_kernel(q_ref, k_ref, v_ref, g_ref, dmat_ref, kdec_ref, qdec_ref, gch_ref, gain_ref,
                o_ref, kv_ref, st_ref, *, chunk, n_chunks):
    C = chunk
    D = HEAD_DIM
    g_chunk = gch_ref[...]

    def rows(n):
        return pl.ds(pl.multiple_of(n * C, C), C)

    kf_col = kdec_ref[:, 0:D]
    kb_col = kdec_ref[:, D:2 * D]

    U = RET_UNROLL
    assert n_chunks % U == 0

    def summaries(i, carry):
        ns = [i * U + u for u in range(U)]
        ks = [k_ref[rows(n), :].astype(F32) for n in ns]
        kfbs = [jnp.concatenate([(k * kf_col).astype(BF16), (k * kb_col).astype(BF16)], axis=1)
                for k in ks]
        kvs = [lax.dot_general(kfb, v_ref[rows(n), :], (((0,), (0,)), ((), ())),
                               preferred_element_type=F32) for kfb, n in zip(kfbs, ns)]
        for n, kv in zip(ns, kvs):
            kv_ref[n] = kv
        return carry

    lax.fori_loop(0, n_chunks // U, summaries, 0)

    def scans(i, carry):
        sf, sb = carry
        nb = n_chunks - 1 - i
        st_ref[i, 0:D, :] = sf.astype(BF16)
        st_ref[nb, D:2 * D, :] = sb.astype(BF16)
        return sf * g_chunk + kv_ref[i, 0:D, :], sb * g_chunk + kv_ref[nb, D:2 * D, :]

    zero = jnp.zeros((D, D), F32)
    lax.fori_loop(0, n_chunks, scans, (zero, zero))

    dmat = dmat_ref[...]
    qf_col = qdec_ref[:, 0:D]
    qb_col = qdec_ref[:, D:2 * D]
    gain = gain_ref[...]

    def outputs(i, carry):
        ns = [i * U + u for u in range(U)]
        qs = [q_ref[rows(n), :] for n in ns]
        ss = [lax.dot_general(q, k_ref[rows(n), :], (((1,), (1,)), ((), ())),
                              preferred_element_type=F32) for q, n in zip(qs, ns)]
        lhss = []
        for q, s in zip(qs, ss):
            qf = q.astype(F32)
            lhss.append(jnp.concatenate(
                [(s * dmat).astype(BF16), (qf * qf_col).astype(BF16), (qf * qb_col).astype(BF16)],
                axis=1))
        ys = [jnp.dot(lhs, jnp.concatenate([v_ref[rows(n), :], st_ref[n]], axis=0),
                      preferred_element_type=F32) for lhs, n in zip(lhss, ns)]
        for n, y in zip(ns, ys):
            gate = g_ref[rows(n), :].astype(F32)
            o_ref[rows(n), :] = (gate * _sigmoid(gate) * _rms(y, gain)).astype(BF16)
        return carry

    lax.fori_loop(0, n_chunks // U, outputs, 0)


def _ret_tables(chunk):
    h = jnp.arange(RET_HEADS, dtype=F32)
    log_g = jnp.log1p(-(2.0 ** (-5.0 - h)))
    pos = jnp.arange(chunk, dtype=F32)
    dmat = jnp.exp(log_g[:, None, None] * jnp.abs(pos[:, None] - pos[None, :])[None])

    def cols(e_first, e_second):
        both = [jnp.exp(log_g[:, None] * e[None, :]) for e in (e_first, e_second)]
        return jnp.concatenate(
            [jnp.broadcast_to(c[:, :, None], (RET_HEADS, chunk, HEAD_DIM)) for c in both], axis=-1)

    kdec = cols(chunk - 1 - pos, pos)
    qdec = cols(pos + 1.0, chunk - pos)
    gch = jnp.broadcast_to(jnp.exp(log_g * chunk)[:, None, None], (RET_HEADS, 1, HEAD_DIM))
    return dmat, kdec, qdec, gch


def _retention(proj, ret_gain, tables, batch, seq):
    T = proj.shape[0]
    C = min(RET_CHUNK, seq)
    n_chunks = seq // C
    dmat, kdec, qdec, gch = tables
    H = RET_HEADS

    def col(off):
        return pl.BlockSpec((seq, HEAD_DIM), lambda b, h: (b, off + h))

    def per_head(*shape):
        return pl.BlockSpec((None,) + shape, lambda b, h: (h, 0, 0))

    return pl.pallas_call(
        functools.partial(_ret_kernel, chunk=C, n_chunks=n_chunks),
        grid=(batch, H),
        in_specs=[
            col(0), col(H), col(2 * H), col(3 * H),
            per_head(C, C), per_head(C, 2 * HEAD_DIM), per_head(C, 2 * HEAD_DIM),
            per_head(1, HEAD_DIM), per_head(1, HEAD_DIM),
        ],
        out_specs=pl.BlockSpec((seq, HEAD_DIM), lambda b, h: (b, h)),
        out_shape=jax.ShapeDtypeStruct((T, RET_WIDTH), BF16),
        scratch_shapes=[pltpu.VMEM((n_chunks, 2 * HEAD_DIM, HEAD_DIM), F32),
                        pltpu.VMEM((n_chunks, 2 * HEAD_DIM, HEAD_DIM), BF16)],
        compiler_params=_params("parallel", "parallel"),
        name="retention",
    )(proj, proj, proj, proj, dmat, kdec, qdec, gch, ret_gain)


def _segment_carries(h_end, a_end, reverse):
    c = jnp.zeros((1, h_end.shape[1]), F32)
    cs = [c]
    order = range(LRU_SEGS - 1, 0, -1) if reverse else range(0, LRU_SEGS - 1)
    for j in order:
        c = h_end[j:j + 1, :] + a_end[j:j + 1, :] * c
        cs.append(c)
    return jnp.concatenate(cs[::-1] if reverse else cs, axis=0)


def _lru_kernel(u_ref, cw_ref, cb_ref, w4_ref, b4_ref, lam_ref, o_ref,
                uf_ref, hf_ref, af_ref, ab_ref, bb_ref, *, seq):
    W = LRU_BLOCK
    NS = LRU_SEGS
    L = seq // NS
    P = L + LRU_SLOT_PAD
    TT = min(LRU_TT, L)
    TW = min(LRU_TW, L)
    n_tiles = L // TT
    taps = CONV_WIDTH

    for j in range(NS):
        def widen(r, carry, j=j):
            src = pl.ds(pl.multiple_of(j * L + r * TW, TW), TW)
            dst = pl.ds(pl.multiple_of(j * P + LRU_LEAD + r * TW, V7X_SUBLANES), TW)
            uf_ref[dst, :] = u_ref[src, :].astype(F32)
            return carry
        lax.fori_loop(0, L // TW, widen, 0)
    zeros8 = jnp.zeros((V7X_SUBLANES, W), F32)
    for j in range(NS):
        data = j * P + LRU_LEAD
        before = zeros8 if j == 0 else uf_ref[data - P + L - 8:data - P + L, :]
        after = zeros8 if j == NS - 1 else uf_ref[data + P:data + P + 8, :]
        uf_ref[data - 8:data, :] = before
        uf_ref[data + L:data + L + 8, :] = after

    cw = cw_ref[...]
    cb = cb_ref[...]
    w4 = w4_ref[...]
    b4 = b4_ref[...]
    z = -lam_ref[...]
    softplus = jnp.maximum(z, 0.0) + jnp.log1p(jnp.exp(-jnp.abs(z)))
    c_half = (0.5 * LRU_C) * softplus

    def seg_rows(m):
        return pl.ds(pl.multiple_of(m * (TT * NS), TT * NS), TT * NS)

    def step(v, t):
        return v[t * NS:(t + 1) * NS, :]

    def gates_fwd(m, carry):
        a_run, h = carry
        t0 = m * TT
        useg = [uf_ref[pl.ds(LRU_LEAD - CONV_PAD_LEFT + t0 + i, NS, stride=P), :]
                for i in range(TT + taps - 1)]
        ucs = []
        for t in range(TT):
            uc = cb
            for k in range(taps):
                uc = uc + cw[k:k + 1, :] * useg[t + k]
            ucs.append(uc)
        uc = jnp.concatenate(ucs, axis=0)
        th = jnp.tanh(jnp.dot(uc.astype(BF16), w4, preferred_element_type=F32) + b4)
        hu = 0.5 * uc
        ab = []
        for d in range(2):
            t_r = th[:, (2 * d) * W:(2 * d + 1) * W]
            t_i = th[:, (2 * d + 1) * W:(2 * d + 2) * W]
            ch = c_half[d:d + 1, :]
            nla = ch + ch * t_r
            a = jnp.exp2(nla * NEG_LOG2_E)
            om = jnp.tanh(nla) * (a * a + 1.0)
            root = jnp.where(om > 0.0, om * lax.rsqrt(om), 0.0)
            b = root * (hu + hu * t_i)
            ab.append((a, b))
        (a_f, b_f), (a_b, b_b) = ab
        ab_ref[seg_rows(m), :] = a_b
        bb_ref[seg_rows(m), :] = b_b
        a_out, h_out = [], []
        for t in range(TT):
            a_t = step(a_f, t)
            a_run = a_t * a_run
            h = a_t * h + step(b_f, t)
            a_out.append(a_run)
            h_out.append(h)
        af_ref[seg_rows(m), :] = jnp.concatenate(a_out, axis=0)
        hf_ref[seg_rows(m), :] = jnp.concatenate(h_out, axis=0)
        return a_run, h

    ones = jnp.ones((NS, W), F32)
    zeros = jnp.zeros((NS, W), F32)
    af_end, hf_end = lax.fori_loop(0, n_tiles, gates_fwd, (ones, zeros), unroll=LRU_UNROLL)

    def bwd(i, carry):
        a_run, h = carry
        m = n_tiles - 1 - i
        a_b = ab_ref[seg_rows(m), :]
        b_b = bb_ref[seg_rows(m), :]
        a_out, h_out = [None] * TT, [None] * TT
        for t in range(TT - 1, -1, -1):
            a_t = step(a_b, t)
            a_run = a_t * a_run
            h = a_t * h + step(b_b, t)
            a_out[t] = a_run
            h_out[t] = h
        ab_ref[seg_rows(m), :] = jnp.concatenate(a_out, axis=0)
        bb_ref[seg_rows(m), :] = jnp.concatenate(h_out, axis=0)
        return a_run, h

    ab_end, hb_end = lax.fori_loop(0, n_tiles, bwd, (ones, zeros), unroll=LRU_UNROLL)

    cf = _segment_carries(hf_end, af_end, reverse=False)
    cbk = _segment_carries(hb_end, ab_end, reverse=True)

    def combine(m, carry):
        hf = hf_ref[seg_rows(m), :]
        af = af_ref[seg_rows(m), :]
        hb = bb_ref[seg_rows(m), :]
        ab = ab_ref[seg_rows(m), :]
        for t in range(TT):
            tot = (step(hf, t) + step(af, t) * cf) + (step(hb, t) + step(ab, t) * cbk)
            uf_ref[pl.ds(LRU_LEAD + m * TT + t, NS, stride=P), :] = tot
        return carry

    lax.fori_loop(0, n_tiles, combine, 0, unroll=LRU_UNROLL)

    for j in range(NS):
        def narrow(r, carry, j=j):
            src = pl.ds(pl.multiple_of(j * P + LRU_LEAD + r * TW, V7X_SUBLANES), TW)
            dst = pl.ds(pl.multiple_of(j * L + r * TW, TW), TW)
            o_ref[dst, :] = uf_ref[src, :].astype(BF16)
            return carry
        lax.fori_loop(0, L // TW, narrow, 0)


def _lru(proj, conv_w, conv_b, w4, b4, lam2, batch, seq):
    T = proj.shape[0]
    G, W = LRU_BLOCKS, LRU_BLOCK
    u_col = (4 * RET_WIDTH) // W
    slot = seq // LRU_SEGS + LRU_SLOT_PAD
    assert seq % (LRU_SEGS * 16) == 0 and (slot // V7X_SUBLANES) % 2 == 1

    def per_group(*shape):
        return pl.BlockSpec((None,) + shape, lambda b, g: (g, 0, 0))

    return pl.pallas_call(
        functools.partial(_lru_kernel, seq=seq),
        grid=(batch, G),
        in_specs=[
            pl.BlockSpec((seq, W), lambda b, g: (b, u_col + g)),
            pl.BlockSpec((CONV_WIDTH, W), lambda b, g: (0, g)),
            per_group(1, W), per_group(W, 4 * W), per_group(1, 4 * W), per_group(2, W),
        ],
        out_specs=pl.BlockSpec((seq, W), lambda b, g: (b, g)),
        out_shape=jax.ShapeDtypeStruct((T, LRU_WIDTH), BF16),
        scratch_shapes=[pltpu.VMEM((LRU_SEGS * slot, W), F32)] + [pltpu.VMEM((seq, W), F32)] * 4,
        compiler_params=_params("parallel", "parallel"),
        name="rglru",
    )(proj, conv_w, conv_b, w4, b4, lam2)


def _outproj_kernel(ret_ref, lru_ref, uy_ref, x_ref, w_ref, lgain_ref, pgain_ref, o_ref):
    lru = _rms(lru_ref[...].astype(F32), lgain_ref[...]) * _gelu_tanh(uy_ref[...].astype(F32))
    mix = jnp.dot(ret_ref[...], w_ref[0:RET_WIDTH, :], preferred_element_type=F32)
    mix += jnp.dot(lru.astype(BF16), w_ref[RET_WIDTH:RET_WIDTH + LRU_WIDTH, :],
                   preferred_element_type=F32)
    o_ref[...] = x_ref[...] + _rms(mix, pgain_ref[...])


def _outproj(ret, lru, proj, x2, w_out, lru_gain, post_gain):
    T, D = x2.shape
    tm = min(OUT_TM, T)
    uy_col = (4 * RET_WIDTH + LRU_WIDTH) // LRU_WIDTH
    return pl.pallas_call(
        _outproj_kernel,
        grid=(T // tm,),
        in_specs=[
            pl.BlockSpec((tm, RET_WIDTH), lambda i: (i, 0)),
            pl.BlockSpec((tm, LRU_WIDTH), lambda i: (i, 0)),
            pl.BlockSpec((tm, LRU_WIDTH), lambda i: (i, uy_col)),
            pl.BlockSpec((tm, D), lambda i: (i, 0)),
            pl.BlockSpec((RET_WIDTH + LRU_WIDTH, D), lambda i: (0, 0)),
            pl.BlockSpec((1, LRU_WIDTH), lambda i: (0, 0)),
            pl.BlockSpec((1, D), lambda i: (0, 0)),
        ],
        out_specs=pl.BlockSpec((tm, D), lambda i: (i, 0)),
        out_shape=jax.ShapeDtypeStruct((T, D), F32),
        compiler_params=_params("parallel"),
        name="outproj",
    )(ret, lru, proj, x2, w_out, lru_gain, post_gain)


def _ffn_kernel(x_ref, pre_ref, wg_ref, wu_ref, wd_ref, post_ref, o_ref, hs_ref, acc_ref):
    f = pl.program_id(1)

    @pl.when(f == 0)
    def _():
        hs_ref[...] = _rms(x_ref[...], pre_ref[...]).astype(BF16)
        acc_ref[...] = jnp.zeros_like(acc_ref)

    h = hs_ref[...]
    gate = jnp.dot(h, wg_ref[...], preferred_element_type=F32)
    up = jnp.dot(h, wu_ref[...], preferred_element_type=F32)
    act = (gate * _sigmoid(gate) * up).astype(BF16)
    acc_ref[...] += jnp.dot(act, wd_ref[...], preferred_element_type=F32)

    @pl.when(f == pl.num_programs(1) - 1)
    def _():
        o_ref[...] = x_ref[...] + _rms(acc_ref[...], post_ref[...])


def _ffn(x2, pre_gain, w_gate, w_up, w_down, post_gain):
    T, D = x2.shape
    F = w_gate.shape[1]
    tm = min(FFN_TM, T)
    tf = FFN_TF
    return pl.pallas_call(
        _ffn_kernel,
        grid=(T // tm, F // tf),
        in_specs=[
            pl.BlockSpec((tm, D), lambda i, f: (i, 0)),
            pl.BlockSpec((1, D), lambda i, f: (0, 0)),
            pl.BlockSpec((D, tf), lambda i, f: (0, f)),
            pl.BlockSpec((D, tf), lambda i, f: (0, f)),
            pl.BlockSpec((tf, D), lambda i, f: (f, 0)),
            pl.BlockSpec((1, D), lambda i, f: (0, 0)),
        ],
        out_specs=pl.BlockSpec((tm, D), lambda i, f: (i, 0)),
        out_shape=jax.ShapeDtypeStruct((T, D), F32),
        scratch_shapes=[pltpu.VMEM((tm, D), BF16), pltpu.VMEM((tm, D), F32)],
        compiler_params=_params("parallel", "arbitrary"),
        name="swiglu",
    )(x2, pre_gain, w_gate, w_up, w_down, post_gain)


def _rope_tables(seq):
    d = HEAD_DIM
    inv = ROPE_BASE ** (-jnp.arange(0, d, 2, dtype=F32) / d)
    ang = jnp.arange(seq, dtype=F32)[:, None] * inv[None, :]
    cos, sin = jnp.cos(ang), jnp.sin(ang)
    return jnp.concatenate([cos, cos], axis=-1), jnp.concatenate([-sin, sin], axis=-1)


def _prepare_layer(l, norm_pre_mix, w_in, ret_norm, conv_w, conv_b,
                   lru_wa_fwd, lru_ba_fwd, lru_wx_fwd, lru_bx_fwd, lru_lam_fwd,
                   lru_wa_bwd, lru_ba_bwd, lru_wx_bwd, lru_bx_bwd, lru_lam_bwd,
                   lru_norm, w_out, norm_post_mix, norm_pre_ffn, w_gate, w_up, w_down, norm_post_ffn):
    G, W = LRU_BLOCKS, LRU_BLOCK
    grp = lambda v: v[l].reshape(G, 1, W)
    return dict(
        pre_mix=norm_pre_mix[l][None, :],
        w_in=w_in[l].astype(BF16),
        ret_gain=ret_norm[l][:, None, :],
        conv_w=conv_w[l],
        conv_b=grp(conv_b),
        w4=(0.5 * jnp.concatenate([lru_wa_fwd[l], lru_wx_fwd[l], lru_wa_bwd[l], lru_wx_bwd[l]],
                                  axis=-1)).astype(BF16),
        b4=0.5 * jnp.concatenate([grp(lru_ba_fwd), grp(lru_bx_fwd), grp(lru_ba_bwd), grp(lru_bx_bwd)],
                                 axis=-1),
        lam2=jnp.concatenate([grp(lru_lam_fwd), grp(lru_lam_bwd)], axis=1),
        lru_gain=lru_norm[l][None, :],
        w_out=w_out[l].astype(BF16),
        post_mix=norm_post_mix[l][None, :],
        pre_ffn=norm_pre_ffn[l][None, :],
        w_gate=w_gate[l].astype(BF16),
        w_up=w_up[l].astype(BF16),
        w_down=w_down[l].astype(BF16),
        post_ffn=norm_post_ffn[l][None, :],
    )


def _trunk(x, layers, ret_tables):
    B, S, D = x.shape
    cosf, sinf = _rope_tables(S)
    x2 = x.reshape(B * S, D)
    for p in layers:
        proj = _inproj(x2, p["pre_mix"], p["w_in"], cosf, sinf, S)
        ret = _retention(proj, p["ret_gain"], ret_tables, B, S)
        lru = _lru(proj, p["conv_w"], p["conv_b"], p["w4"], p["b4"], p["lam2"], B, S)
        x2 = _outproj(ret, lru, proj, x2, p["w_out"], p["lru_gain"], p["post_mix"])
        x2 = _ffn(x2, p["pre_ffn"], p["w_gate"], p["w_up"], p["w_down"], p["post_ffn"])
    return x2.reshape(B, S, D)


def kernel(x_prompt, x_sample, norm_pre_mix, w_in, ret_norm, conv_w, conv_b, lru_wa_fwd, lru_ba_fwd, lru_wx_fwd, lru_bx_fwd, lru_lam_fwd, lru_wa_bwd, lru_ba_bwd, lru_wx_bwd, lru_bx_bwd, lru_lam_bwd, lru_norm, w_out, norm_post_mix, norm_pre_ffn, w_gate, w_up, w_down, norm_post_ffn):
    weights = (norm_pre_mix, w_in, ret_norm, conv_w, conv_b,
               lru_wa_fwd, lru_ba_fwd, lru_wx_fwd, lru_bx_fwd, lru_lam_fwd,
               lru_wa_bwd, lru_ba_bwd, lru_wx_bwd, lru_bx_bwd, lru_lam_bwd,
               lru_norm, w_out, norm_post_mix, norm_pre_ffn, w_gate, w_up, w_down, norm_post_ffn)
    depth = w_in.shape[0]
    layers = [_prepare_layer(l, *weights) for l in range(depth)]
    ret_tables = _ret_tables(RET_CHUNK)
    return (_trunk(x_prompt, layers, ret_tables), _trunk(x_sample, layers, ret_tables))
```

```python
import functools

import jax
import jax.numpy as jnp
from jax import lax
from jax.experimental import pallas as pl
from jax.experimental.pallas import tpu as pltpu

F32 = jnp.float32
BF16 = jnp.bfloat16

RET_HEADS = 8
HEAD_DIM = 128
RET_WIDTH = RET_HEADS * HEAD_DIM
LRU_BLOCKS = 8
LRU_BLOCK = 128
LRU_WIDTH = LRU_BLOCKS * LRU_BLOCK
CONV_WIDTH = 4
CONV_PAD_LEFT = 2
LRU_C = 8.0
ROPE_BASE = 10000.0
EPS = 1e-6

V7X_LANES = 128
V7X_SUBLANES = 8
V7X_VMEM_LIMIT_BYTES = 56 * 1024 * 1024

IN_TM = 1024
IN_TN = 1024
IN_SUB = 256
ROT_TILES = 2
OUT_TM = 512
OUT_SUB = 256
FFN_TM = 512
FFN_TF = 512
RET_CHUNK = 128
RET_UNROLL = 8
LRU_SEGS = V7X_SUBLANES
LRU_LEAD = V7X_SUBLANES
LRU_SLOT_PAD = 24
LRU_TT = 16
LRU_TW = 128
LRU_UNROLL = 4
NEG_LOG2_E = -1.4426950408889634


def _params(*sem):
    return pltpu.CompilerParams(dimension_semantics=sem, vmem_limit_bytes=V7X_VMEM_LIMIT_BYTES)


def _rms(x, gain):
    ms = jnp.mean(x * x, axis=-1, keepdims=True)
    return x * lax.rsqrt(ms + EPS) * gain


def _gelu_tanh(x):
    c = 0.7978845608028654
    return 0.5 * x * (1.0 + jnp.tanh(c * (x + 0.044715 * (x * x * x))))


def _inproj_kernel(x_ref, gain_ref, w_ref, ra_ref, rb_ref, o_ref, hs_ref):
    @pl.when(pl.program_id(1) == 0)
    def _():
        hs_ref[...] = _rms(x_ref[...], gain_ref[...]).astype(BF16)

    sub = min(IN_SUB, o_ref.shape[0])
    for r in range(o_ref.shape[0] // sub):
        rs = slice(r * sub, (r + 1) * sub)
        acc = jnp.dot(hs_ref[rs, :], w_ref[...], preferred_element_type=F32)
        ra = ra_ref[rs, :]
        rb = rb_ref[rs, :]
        for h in range(IN_TN // HEAD_DIM):
            sl = slice(h * HEAD_DIM, (h + 1) * HEAD_DIM)
            a = acc[:, sl]
            o_ref[rs, sl] = (a * ra + pltpu.roll(a, HEAD_DIM // 2, axis=1) * rb).astype(BF16)


def _inproj(x2, gain, w_in, layer, rot_a, rot_b, seq):
    T, D = x2.shape
    N = w_in.shape[2]
    tm = min(IN_TM, seq)
    pos_blocks = seq // tm
    n_rot = ROT_TILES

    def rot_spec():
        return pl.BlockSpec(
            (None, tm, HEAD_DIM),
            lambda i, j: (jnp.minimum(j, n_rot), jnp.where(j < n_rot, i % pos_blocks, 0), 0))

    return pl.pallas_call(
        _inproj_kernel,
        grid=(T // tm, N // IN_TN),
        in_specs=[
            pl.BlockSpec((tm, D), lambda i, j: (i, 0)),
            pl.BlockSpec((1, D), lambda i, j: (0, 0)),
            pl.BlockSpec((None, D, IN_TN), lambda i, j: (layer, 0, j)),
            rot_spec(),
            rot_spec(),
        ],
        out_specs=pl.BlockSpec((tm, IN_TN), lambda i, j: (i, j)),
        out_shape=jax.ShapeDtypeStruct((T, N), BF16),
        scratch_shapes=[pltpu.VMEM((tm, D), BF16)],
        compiler_params=_params("parallel", "arbitrary"),
        name="inproj",
    )(x2, gain, w_in, rot_a, rot_b)


def _ret_kernel(q_ref, k_ref, v_ref, g_ref, dmat_ref, kdec_ref, qdec_ref, gch_ref, gain_ref,
                o_ref, kv_ref, st_ref, *, chunk, n_chunks):
    C = chunk
    D = HEAD_DIM
    g_chunk = gch_ref[...]

    def rows(n):
        return pl.ds(pl.multiple_of(n * C, C), C)

    kf_col = kdec_ref[:, 0:D]
    kb_col = kdec_ref[:, D:2 * D]

    U = RET_UNROLL
    assert n_chunks % U == 0

    def summaries(i, carry):
        ns = [i * U + u for u in range(U)]
        ks = [k_ref[rows(n), :].astype(F32) for n in ns]
        kfbs = [jnp.concatenate([(k * kf_col).astype(BF16), (k * kb_col).astype(BF16)], axis=1)
                for k in ks]
        kvs = [lax.dot_general(kfb, v_ref[rows(n), :], (((0,), (0,)), ((), ())),
                               preferred_element_type=F32) for kfb, n in zip(kfbs, ns)]
        for n, kv in zip(ns, kvs):
            kv_ref[n] = kv
        return carry

    lax.fori_loop(0, n_chunks // U, summaries, 0)

    def scans(i, carry):
        sf, sb = carry
        nb = n_chunks - 1 - i
        st_ref[i, 0:D, :] = sf.astype(BF16)
        st_ref[nb, D:2 * D, :] = sb.astype(BF16)
        return sf * g_chunk + kv_ref[i, 0:D, :], sb * g_chunk + kv_ref[nb, D:2 * D, :]

    zero = jnp.zeros((D, D), F32)
    lax.fori_loop(0, n_chunks, scans, (zero, zero))

    dmat = dmat_ref[...]
    qf_col = qdec_ref[:, 0:D]
    qb_col = qdec_ref[:, D:2 * D]
    gain = gain_ref[...]

    def outputs(i, carry):
        ns = [i * U + u for u in range(U)]
        qs = [q_ref[rows(n), :] for n in ns]
        ss = [lax.dot_general(q, k_ref[rows(n), :], (((1,), (1,)), ((), ())),
                              preferred_element_type=F32) for q, n in zip(qs, ns)]
        lhss = []
        for q, s in zip(qs, ss):
            qf = q.astype(F32)
            lhss.append(jnp.concatenate(
                [(s * dmat).astype(BF16), (qf * qf_col).astype(BF16), (qf * qb_col).astype(BF16)],
                axis=1))
        ys = [jnp.dot(lhs, jnp.concatenate([v_ref[rows(n), :], st_ref[n]], axis=0),
                      preferred_element_type=F32) for lhs, n in zip(lhss, ns)]
        for n, y in zip(ns, ys):
            half = 0.5 * g_ref[rows(n), :].astype(F32)
            o_ref[rows(n), :] = ((half + half * jnp.tanh(half)) * _rms(y, gain)).astype(BF16)
        return carry

    lax.fori_loop(0, n_chunks // U, outputs, 0)


def _ret_tables(chunk):
    h = jnp.arange(RET_HEADS, dtype=F32)
    log_g = jnp.log1p(-(2.0 ** (-5.0 - h)))
    pos = jnp.arange(chunk, dtype=F32)
    dmat = jnp.exp(log_g[:, None, None] * jnp.abs(pos[:, None] - pos[None, :])[None])

    def cols(e_first, e_second):
        both = [jnp.exp(log_g[:, None] * e[None, :]) for e in (e_first, e_second)]
        return jnp.concatenate(
            [jnp.broadcast_to(c[:, :, None], (RET_HEADS, chunk, HEAD_DIM)) for c in both], axis=-1)

    kdec = cols(chunk - 1 - pos, pos)
    qdec = cols(pos + 1.0, chunk - pos)
    gch = jnp.broadcast_to(jnp.exp(log_g * chunk)[:, None, None], (RET_HEADS, 1, HEAD_DIM))
    return dmat, kdec, qdec, gch


def _retention(proj, ret_gain, tables, batch, seq):
    T = proj.shape[0]
    C = min(RET_CHUNK, seq)
    n_chunks = seq // C
    dmat, kdec, qdec, gch = tables
    H = RET_HEADS

    def col(off):
        return pl.BlockSpec((seq, HEAD_DIM), lambda b, h: (b, off + h))

    def per_head(*shape):
        return pl.BlockSpec((None,) + shape, lambda b, h: (h, 0, 0))

    return pl.pallas_call(
        functools.partial(_ret_kernel, chunk=C, n_chunks=n_chunks),
        grid=(batch, H),
        in_specs=[
            col(0), col(H), col(2 * H), col(3 * H),
            per_head(C, C), per_head(C, 2 * HEAD_DIM), per_head(C, 2 * HEAD_DIM),
            per_head(1, HEAD_DIM), per_head(1, HEAD_DIM),
        ],
        out_specs=pl.BlockSpec((seq, HEAD_DIM), lambda b, h: (b, h)),
        out_shape=jax.ShapeDtypeStruct((T, RET_WIDTH), BF16),
        scratch_shapes=[pltpu.VMEM((n_chunks, 2 * HEAD_DIM, HEAD_DIM), F32),
                        pltpu.VMEM((n_chunks, 2 * HEAD_DIM, HEAD_DIM), BF16)],
        compiler_params=_params("parallel", "parallel"),
        name="retention",
    )(proj, proj, proj, proj, dmat, kdec, qdec, gch, ret_gain)


def _segment_carries(h_end, a_end, reverse):
    c = jnp.zeros((1, h_end.shape[1]), F32)
    cs = [c]
    order = range(LRU_SEGS - 1, 0, -1) if reverse else range(0, LRU_SEGS - 1)
    for j in order:
        c = h_end[j:j + 1, :] + a_end[j:j + 1, :] * c
        cs.append(c)
    return jnp.concatenate(cs[::-1] if reverse else cs, axis=0)


def _lru_kernel(u_ref, cw_ref, cb_ref, w4_ref, b4_ref, lam_ref, o_ref,
                uf_ref, hf_ref, af_ref, ab_ref, bb_ref, *, seq):
    W = LRU_BLOCK
    NS = LRU_SEGS
    L = seq // NS
    P = L + LRU_SLOT_PAD
    TT = min(LRU_TT, L)
    TW = min(LRU_TW, L)
    n_tiles = L // TT
    taps = CONV_WIDTH

    for j in range(NS):
        def widen(r, carry, j=j):
            src = pl.ds(pl.multiple_of(j * L + r * TW, TW), TW)
            dst = pl.ds(pl.multiple_of(j * P + LRU_LEAD + r * TW, V7X_SUBLANES), TW)
            uf_ref[dst, :] = u_ref[src, :].astype(F32)
            return carry
        lax.fori_loop(0, L // TW, widen, 0, unroll=True)
    zeros8 = jnp.zeros((V7X_SUBLANES, W), F32)
    for j in range(NS):
        data = j * P + LRU_LEAD
        before = zeros8 if j == 0 else uf_ref[data - P + L - 8:data - P + L, :]
        after = zeros8 if j == NS - 1 else uf_ref[data + P:data + P + 8, :]
        uf_ref[data - 8:data, :] = before
        uf_ref[data + L:data + L + 8, :] = after

    cw = cw_ref[...]
    cb = cb_ref[...]
    w4 = w4_ref[...]
    b4 = b4_ref[...]
    z = -lam_ref[...]
    softplus = jnp.maximum(z, 0.0) + jnp.log1p(jnp.exp(-jnp.abs(z)))
    c_half = (0.5 * LRU_C) * softplus

    def seg_rows(m):
        return pl.ds(pl.multiple_of(m * (TT * NS), TT * NS), TT * NS)

    def step(v, t):
        return v[t * NS:(t + 1) * NS, :]

    def gates_fwd(m, carry):
        a_run, h = carry
        t0 = m * TT
        useg = [uf_ref[pl.ds(LRU_LEAD - CONV_PAD_LEFT + t0 + i, NS, stride=P), :]
                for i in range(TT + taps - 1)]
        ucs = []
        for t in range(TT):
            uc = cb
            for k in range(taps):
                uc = uc + cw[k:k + 1, :] * useg[t + k]
            ucs.append(uc)
        uc = jnp.concatenate(ucs, axis=0)
        th = jnp.tanh(jnp.dot(uc.astype(BF16), w4, preferred_element_type=F32) + b4)
        hu = 0.5 * uc
        ab = []
        for d in range(2):
            t_r = th[:, (2 * d) * W:(2 * d + 1) * W]
            t_i = th[:, (2 * d + 1) * W:(2 * d + 2) * W]
            ch = c_half[d:d + 1, :]
            nla = ch + ch * t_r
            a = jnp.exp2(nla * NEG_LOG2_E)
            om = jnp.tanh(nla) * (a * a + 1.0)
            root = jnp.where(om > 0.0, om * lax.rsqrt(om), 0.0)
            b = root * (hu + hu * t_i)
            ab.append((a, b))
        (a_f, b_f), (a_b, b_b) = ab
        ab_ref[seg_rows(m), :] = a_b
        bb_ref[seg_rows(m), :] = b_b
        a_out, h_out = [], []
        for t in range(TT):
            a_t = step(a_f, t)
            a_run = a_t * a_run
            h = a_t * h + step(b_f, t)
            a_out.append(a_run)
            h_out.append(h)
        af_ref[seg_rows(m), :] = jnp.concatenate(a_out, axis=0)
        hf_ref[seg_rows(m), :] = jnp.concatenate(h_out, axis=0)
        return a_run, h

    ones = jnp.ones((NS, W), F32)
    zeros = jnp.zeros((NS, W), F32)
    af_end, hf_end = lax.fori_loop(0, n_tiles, gates_fwd, (ones, zeros), unroll=LRU_UNROLL)

    def bwd(i, carry):
        a_run, h = carry
        m = n_tiles - 1 - i
        a_b = ab_ref[seg_rows(m), :]
        b_b = bb_ref[seg_rows(m), :]
        a_out, h_out = [None] * TT, [None] * TT
        half = TT // 2
        a_lo, h_lo = ones, zeros
        for t in range(TT - 1, half - 1, -1):
            a_t = step(a_b, t)
            a_run = a_t * a_run
            h = a_t * h + step(b_b, t)
            a_out[t] = a_run
            h_out[t] = h
            a_t = step(a_b, t - half)
            a_lo = a_t * a_lo
            h_lo = a_t * h_lo + step(b_b, t - half)
            a_out[t - half] = a_lo
            h_out[t - half] = h_lo
        for t in range(half):
            h_out[t] = h_out[t] + a_out[t] * h
            a_out[t] = a_out[t] * a_run
        a_run, h = a_out[0], h_out[0]
        ab_ref[seg_rows(m), :] = jnp.concatenate(a_out, axis=0)
        bb_ref[seg_rows(m), :] = jnp.concatenate(h_out, axis=0)
        return a_run, h

    ab_end, hb_end = lax.fori_loop(0, n_tiles, bwd, (ones, zeros), unroll=LRU_UNROLL)

    cf = _segment_carries(hf_end, af_end, reverse=False)
    cbk = _segment_carries(hb_end, ab_end, reverse=True)

    def combine(m, carry):
        hf = hf_ref[seg_rows(m), :]
        af = af_ref[seg_rows(m), :]
        hb = bb_ref[seg_rows(m), :]
        ab = ab_ref[seg_rows(m), :]
        for t in range(TT):
            tot = (step(hf, t) + step(af, t) * cf) + (step(hb, t) + step(ab, t) * cbk)
            uf_ref[pl.ds(LRU_LEAD + m * TT + t, NS, stride=P), :] = tot
        return carry

    lax.fori_loop(0, n_tiles, combine, 0, unroll=LRU_UNROLL)

    for j in range(NS):
        def narrow(r, carry, j=j):
            src = pl.ds(pl.multiple_of(j * P + LRU_LEAD + r * TW, V7X_SUBLANES), TW)
            dst = pl.ds(pl.multiple_of(j * L + r * TW, TW), TW)
            o_ref[dst, :] = uf_ref[src, :].astype(BF16)
            return carry
        lax.fori_loop(0, L // TW, narrow, 0, unroll=True)


def _lru(proj, conv_w, conv_b, w4, b4, lam2, batch, seq):
    T = proj.shape[0]
    G, W = LRU_BLOCKS, LRU_BLOCK
    u_col = (4 * RET_WIDTH) // W
    slot = seq // LRU_SEGS + LRU_SLOT_PAD
    assert seq % (LRU_SEGS * 16) == 0 and (slot // V7X_SUBLANES) % 2 == 1

    def per_group(*shape):
        return pl.BlockSpec((None,) + shape, lambda b, g: (g, 0, 0))

    return pl.pallas_call(
        functools.partial(_lru_kernel, seq=seq),
        grid=(batch, G),
        in_specs=[
            pl.BlockSpec((seq, W), lambda b, g: (b, u_col + g)),
            pl.BlockSpec((CONV_WIDTH, W), lambda b, g: (0, g)),
            per_group(1, W), per_group(W, 4 * W), per_group(1, 4 * W), per_group(2, W),
        ],
        out_specs=pl.BlockSpec((seq, W), lambda b, g: (b, g)),
        out_shape=jax.ShapeDtypeStruct((T, LRU_WIDTH), BF16),
        scratch_shapes=[pltpu.VMEM((LRU_SEGS * slot, W), F32)] + [pltpu.VMEM((seq, W), F32)] * 4,
        compiler_params=_params("parallel", "parallel"),
        name="rglru",
    )(proj, conv_w, conv_b, w4, b4, lam2)


def _outproj_kernel(ret_ref, lru_ref, uy_ref, x_ref, w_ref, lgain_ref, pgain_ref, o_ref):
    for r in range(o_ref.shape[0] // OUT_SUB):
        rs = slice(r * OUT_SUB, (r + 1) * OUT_SUB)
        lru = (_rms(lru_ref[rs, :].astype(F32), lgain_ref[...])
               * _gelu_tanh(uy_ref[rs, :].astype(F32)))
        lhs = jnp.concatenate([ret_ref[rs, :], lru.astype(BF16)], axis=1)
        mix = jnp.dot(lhs, w_ref[...], preferred_element_type=F32)
        o_ref[rs, :] = x_ref[rs, :] + _rms(mix, pgain_ref[...])


def _outproj(ret, lru, proj, x2, w_out, layer, lru_gain, post_gain):
    T, D = x2.shape
    tm = min(OUT_TM, T)
    uy_col = (4 * RET_WIDTH + LRU_WIDTH) // LRU_WIDTH
    return pl.pallas_call(
        _outproj_kernel,
        grid=(T // tm,),
        in_specs=[
            pl.BlockSpec((tm, RET_WIDTH), lambda i: (i, 0)),
            pl.BlockSpec((tm, LRU_WIDTH), lambda i: (i, 0)),
            pl.BlockSpec((tm, LRU_WIDTH), lambda i: (i, uy_col)),
            pl.BlockSpec((tm, D), lambda i: (i, 0)),
            pl.BlockSpec((None, RET_WIDTH + LRU_WIDTH, D), lambda i: (layer, 0, 0)),
            pl.BlockSpec((1, LRU_WIDTH), lambda i: (0, 0)),
            pl.BlockSpec((1, D), lambda i: (0, 0)),
        ],
        out_specs=pl.BlockSpec((tm, D), lambda i: (i, 0)),
        out_shape=jax.ShapeDtypeStruct((T, D), F32),
        compiler_params=_params("parallel"),
        name="outproj",
    )(ret, lru, proj, x2, w_out, lru_gain, post_gain)


def _ffn_kernel(x_ref, pre_ref, wg_ref, wu_ref, wd_ref, post_ref, o_ref, hs_ref):
    f = pl.program_id(1)

    @pl.when(f == 0)
    def _():
        hs_ref[...] = _rms(x_ref[...], pre_ref[...]).astype(BF16)
        o_ref[...] = jnp.zeros_like(o_ref)

    h = hs_ref[...]
    gate = jnp.dot(h, wg_ref[...], preferred_element_type=F32)
    up = jnp.dot(h, wu_ref[...], preferred_element_type=F32)
    half = 0.5 * gate
    act = ((half + half * jnp.tanh(half)) * up).astype(BF16)
    o_ref[...] += jnp.dot(act, wd_ref[...], preferred_element_type=F32)

    @pl.when(f == pl.num_programs(1) - 1)
    def _():
        o_ref[...] = x_ref[...] + _rms(o_ref[...], post_ref[...])


def _ffn(x2, pre_gain, w_gate, w_up, w_down, layer, post_gain):
    T, D = x2.shape
    F = w_gate.shape[2]
    tm = min(FFN_TM, T)
    tf = FFN_TF
    return pl.pallas_call(
        _ffn_kernel,
        grid=(T // tm, F // tf),
        in_specs=[
            pl.BlockSpec((tm, D), lambda i, f: (i, 0)),
            pl.BlockSpec((1, D), lambda i, f: (0, 0)),
            pl.BlockSpec((None, D, tf), lambda i, f: (layer, 0, f)),
            pl.BlockSpec((None, D, tf), lambda i, f: (layer, 0, f)),
            pl.BlockSpec((None, tf, D), lambda i, f: (layer, f, 0)),
            pl.BlockSpec((1, D), lambda i, f: (0, 0)),
        ],
        out_specs=pl.BlockSpec((tm, D), lambda i, f: (i, 0)),
        out_shape=jax.ShapeDtypeStruct((T, D), F32),
        scratch_shapes=[pltpu.VMEM((tm, D), BF16)],
        compiler_params=_params("parallel", "arbitrary"),
        name="swiglu",
    )(x2, pre_gain, w_gate, w_up, w_down, post_gain)


def _rope_tables(seq):
    d = HEAD_DIM
    inv = ROPE_BASE ** (-jnp.arange(0, d, 2, dtype=F32) / d)
    ang = jnp.arange(seq, dtype=F32)[:, None] * inv[None, :]
    cos, sin = jnp.cos(ang), jnp.sin(ang)
    cosf = jnp.concatenate([cos, cos], axis=-1)
    sinf = jnp.concatenate([-sin, sin], axis=-1)
    k_scale = HEAD_DIM ** -0.5
    rot_a = jnp.stack([cosf, cosf * k_scale, jnp.ones_like(cosf)])
    rot_b = jnp.stack([sinf, sinf * k_scale, jnp.zeros_like(sinf)])
    return rot_a, rot_b


def _prepare_layer(l, norm_pre_mix, w_in, ret_norm, conv_w, conv_b,
                   lru_wa_fwd, lru_ba_fwd, lru_wx_fwd, lru_bx_fwd, lru_lam_fwd,
                   lru_wa_bwd, lru_ba_bwd, lru_wx_bwd, lru_bx_bwd, lru_lam_bwd,
                   lru_norm, w_out, norm_post_mix, norm_pre_ffn, w_gate, w_up, w_down, norm_post_ffn):
    G, W = LRU_BLOCKS, LRU_BLOCK
    grp = lambda v: v[l].reshape(G, 1, W)
    return dict(
        layer=l,
        pre_mix=norm_pre_mix[l][None, :],
        w_in=w_in,
        ret_gain=ret_norm[l][:, None, :],
        conv_w=conv_w[l],
        conv_b=grp(conv_b),
        w4=(0.5 * jnp.concatenate([lru_wa_fwd[l], lru_wx_fwd[l], lru_wa_bwd[l], lru_wx_bwd[l]],
                                  axis=-1)).astype(BF16),
        b4=0.5 * jnp.concatenate([grp(lru_ba_fwd), grp(lru_bx_fwd), grp(lru_ba_bwd), grp(lru_bx_bwd)],
                                 axis=-1),
        lam2=jnp.concatenate([grp(lru_lam_fwd), grp(lru_lam_bwd)], axis=1),
        lru_gain=lru_norm[l][None, :],
        w_out=w_out,
        post_mix=norm_post_mix[l][None, :],
        pre_ffn=norm_pre_ffn[l][None, :],
        w_gate=w_gate,
        w_up=w_up,
        w_down=w_down,
        post_ffn=norm_post_ffn[l][None, :],
    )


def _trunk(x, layers, ret_tables):
    B, S, D = x.shape
    rot_a, rot_b = _rope_tables(S)
    x2 = x.reshape(B * S, D)
    for p in layers:
        l = p["layer"]
        proj = _inproj(x2, p["pre_mix"], p["w_in"], l, rot_a, rot_b, S)
        ret = _retention(proj, p["ret_gain"], ret_tables, B, S)
        lru = _lru(proj, p["conv_w"], p["conv_b"], p["w4"], p["b4"], p["lam2"], B, S)
        x2 = _outproj(ret, lru, proj, x2, p["w_out"], l, p["lru_gain"], p["post_mix"])
        x2 = _ffn(x2, p["pre_ffn"], p["w_gate"], p["w_up"], p["w_down"], l, p["post_ffn"])
    return x2.reshape(B, S, D)


def kernel(x_prompt, x_sample, norm_pre_mix, w_in, ret_norm, conv_w, conv_b, lru_wa_fwd, lru_ba_fwd, lru_wx_fwd, lru_bx_fwd, lru_lam_fwd, lru_wa_bwd, lru_ba_bwd, lru_wx_bwd, lru_bx_bwd, lru_lam_bwd, lru_norm, w_out, norm_post_mix, norm_pre_ffn, w_gate, w_up, w_down, norm_post_ffn):
    w_in, w_out, w_gate, w_up, w_down = (w.astype(BF16) for w in (w_in, w_out, w_gate, w_up, w_down))
    weights = (norm_pre_mix, w_in, ret_norm, conv_w, conv_b,
               lru_wa_fwd, lru_ba_fwd, lru_wx_fwd, lru_bx_fwd, lru_lam_fwd,
               lru_wa_bwd, lru_ba_bwd, lru_wx_bwd, lru_bx_bwd, lru_lam_bwd,
               lru_norm, w_out, norm_post_mix, norm_pre_ffn, w_gate, w_up, w_down, norm_post_ffn)
    depth = w_in.shape[0]
    layers = [_prepare_layer(l, *weights) for l in range(depth)]
    ret_tables = _ret_tables(RET_CHUNK)
    return (_trunk(x_prompt, layers, ret_tables), _trunk(x_sample, layers, ret_tables))
```
